```python
import math
import jax, jax.numpy as jnp
from jax import lax
import numpy as np

D_MODEL = 1024
BATCH = 4
SEQ = 4096
DEPTH = 2

EPS = 1e-6
GLA_HEADS = 4
GLA_HDK = 48
GLA_HDV = 96
GLA_DK = GLA_HEADS * GLA_HDK
GLA_DV = GLA_HEADS * GLA_HDV
GLA_LR = 16
GLA_TAU = 16.0
GLA_CHUNK = 64
SB_HEADS = 6
SB_HD = 64
SB_W = SB_HEADS * SB_HD
SB_BLOCK = 128
POOL_WINDOWS = (2, 4, 8, 16)
POOL_GROUPS = 4
POOL_GC = 64
POOL_W = POOL_GROUPS * POOL_GC
POOL_MAX = 16
D_MIX = GLA_DV + SB_W + POOL_W
PROJ_SIZES = (GLA_DK, GLA_DK, GLA_DV, GLA_DV, GLA_LR, SB_W, SB_W, SB_W, POOL_W)
D_PROJ = GLA_DK * 2 + GLA_DV * 2 + GLA_LR + SB_W * 3 + POOL_W
MOE_GROUPS = 4
MOE_EPG = 8
N_EXPERTS = MOE_GROUPS * MOE_EPG
MOE_TOPK = 2
D_EXPERT = 256
PLE_DIM = 256

kernel_name = "hybrid_gla_stickbreak_pool_hmoe"


def rmsnorm(x, g):
    xf = x.astype(jnp.float32)
    y = xf * lax.rsqrt(jnp.mean(xf * xf, axis=-1, keepdims=True) + EPS)
    return (y * g.astype(jnp.float32)).astype(x.dtype)


def gla_mixer(q, k, v, g_out, lr, w_lr2, b_lr, norm_g):
    B, S, _ = q.shape
    n = S // GLA_CHUNK
    f32 = jnp.float32
    log_a = jax.nn.log_sigmoid((lr @ w_lr2 + b_lr).astype(f32)) / GLA_TAU

    def heads(t, dh):
        return t.reshape(B, n, GLA_CHUNK, GLA_HEADS, dh).transpose(0, 3, 1, 2, 4).astype(f32)

    qh = heads(q, GLA_HDK) * (GLA_HDK ** -0.5)
    kh = heads(k, GLA_HDK)
    vh = heads(v, GLA_HDV)
    b = jnp.cumsum(heads(log_a, GLA_HDK), axis=3)
    b_last = b[:, :, :, -1:, :]
    q_dec = qh * jnp.exp(b)
    k_in = kh * jnp.exp(-b)
    k_out = kh * jnp.exp(b_last - b)
    causal = jnp.tril(jnp.ones((GLA_CHUNK, GLA_CHUNK), dtype=bool))
    att = jnp.where(causal, jnp.einsum('bhncd,bhnsd->bhncs', q_dec, k_in), 0.0)
    o_intra = jnp.einsum('bhncs,bhnsv->bhncv', att, vh)
    kv = jnp.einsum('bhncd,bhncv->bhndv', k_out, vh)
    decay = jnp.exp(b_last[:, :, :, 0, :])

    def step(state, inp):
        kv_n, d_n = inp
        return d_n[..., None] * state + kv_n, state

    s0 = jnp.zeros((B, GLA_HEADS, GLA_HDK, GLA_HDV), f32)
    _, states = lax.scan(step, s0, (jnp.moveaxis(kv, 2, 0), jnp.moveaxis(decay, 2, 0)))
    states = jnp.moveaxis(states, 0, 2)
    o = o_intra + jnp.einsum('bhncd,bhndv->bhncv', q_dec, states)
    o = rmsnorm(o, norm_g)
    o = o.transpose(0, 2, 3, 1, 4).reshape(B, S, GLA_DV)
    return (o * jax.nn.silu(g_out.astype(f32))).astype(v.dtype)


def stick_breaking(q, k, v):
    B, S, _ = q.shape
    f32 = jnp.float32

    def heads(t):
        return t.reshape(B, S, SB_HEADS, SB_HD).transpose(0, 2, 1, 3).astype(f32)

    qh = heads(q) * (SB_HD ** -0.5)
    kh = heads(k)
    vh = heads(v)
    outs = []
    for blk in range(S // SB_BLOCK):
        q0 = blk * SB_BLOCK
        end = q0 + SB_BLOCK
        z = jnp.einsum('bhqd,bhkd->bhqk', qh[:, :, q0:end], kh[:, :, :end])
        qpos = q0 + jnp.arange(SB_BLOCK)[:, None]
        kpos = jnp.arange(end)[None, :]
        mask = kpos < qpos
        log_1m = jnp.where(mask, jax.nn.log_sigmoid(-z), 0.0)
        suffix = lax.cumsum(log_1m, axis=3, reverse=True) - log_1m
        a = jnp.where(mask, jnp.exp(jax.nn.log_sigmoid(z) + suffix), 0.0)
        outs.append(jnp.einsum('bhqk,bhkd->bhqd', a, vh[:, :, :end]))
    o = jnp.concatenate(outs, axis=2)
    return o.transpose(0, 2, 1, 3).reshape(B, S, SB_W).astype(v.dtype)


def pool_mixer(u, w_pool, scale):
    B, S, _ = u.shape
    uf = u.astype(jnp.float32)
    cs = jnp.pad(jnp.cumsum(uf, axis=1), ((0, 0), (POOL_MAX, 0), (0, 0)))
    pos = jnp.arange(S, dtype=jnp.float32)[None, :, None]
    groups = []
    for gi, w in enumerate(POOL_WINDOWS):
        c = slice(gi * POOL_GC, (gi + 1) * POOL_GC)
        wsum = cs[:, POOL_MAX:POOL_MAX + S, c] - cs[:, POOL_MAX - w:POOL_MAX - w + S, c]
        cnt = jnp.minimum(pos + 1.0, float(w))
        groups.append(wsum / cnt - uf[:, :, c])
    pooled = jnp.stack(groups, axis=2)
    y = jnp.einsum('bsgc,gcd->bsgd', pooled, w_pool.astype(jnp.float32)).reshape(B, S, POOL_W)
    return (y * scale.astype(jnp.float32)).astype(u.dtype)


def hier_moe(h, wg, bg, we, be, w_gate, w_up, w_down):
    B, S, D = h.shape
    t = h.reshape(B * S, D)
    f32 = jnp.float32
    g_prob = jax.nn.softmax((t @ wg + bg).astype(f32), axis=-1)
    g_w, g_idx = lax.top_k(g_prob, 1)
    g_onehot = jax.nn.one_hot(g_idx[:, 0], MOE_GROUPS, dtype=f32)
    e_logits = (t @ we + be).astype(f32).reshape(-1, MOE_GROUPS, MOE_EPG)
    e_in = jnp.einsum('tge,tg->te', e_logits, g_onehot)
    e_w, e_idx = lax.top_k(jax.nn.softmax(e_in, axis=-1), MOE_TOPK)
    e_w = e_w / jnp.sum(e_w, axis=-1, keepdims=True) * g_w
    expert_id = g_idx * MOE_EPG + e_idx
    combine = jnp.einsum('tk,tke->te', e_w, jax.nn.one_hot(expert_id, N_EXPERTS, dtype=f32))
    y = jnp.zeros((B * S, D), f32)
    for gi in range(MOE_GROUPS):
        sl = slice(gi * MOE_EPG, (gi + 1) * MOE_EPG)
        a = jnp.einsum('td,edf->tef', t, w_gate[sl])
        b = jnp.einsum('td,edf->tef', t, w_up[sl])
        hid = jax.nn.silu(a.astype(f32)) * b.astype(f32) * combine[:, sl, None]
        y = y + jnp.einsum('tef,efd->td', hid, w_down[sl].astype(f32))
    return y.reshape(B, S, D).astype(h.dtype)


def setup_inputs(seed: int = 0) -> dict:
    key = jax.random.key(seed)
    ks = jax.random.split(key, 24)
    L, D = DEPTH, D_MODEL
    nrm = lambda k, shape, s: jax.random.normal(k, shape, jnp.float32) * s
    gain = lambda k, shape: 1.0 + 0.02 * jax.random.normal(k, shape, jnp.float32)
    return {
        "x": nrm(ks[0], (BATCH, SEQ, D), 1.0),
        "p": nrm(ks[1], (DEPTH, BATCH, SEQ, PLE_DIM), 1.0),
        "norm1_g": gain(ks[2], (L, D)),
        "w_in": nrm(ks[3], (L, D, D_PROJ), D ** -0.5),
        "gla_w_lr2": nrm(ks[4], (L, GLA_LR, GLA_DK), GLA_LR ** -0.5),
        "gla_b_lr": nrm(ks[5], (L, GLA_DK), 0.1),
        "gla_norm_g": gain(ks[6], (L, GLA_HDV)),
        "pool_w": nrm(ks[7], (L, POOL_GROUPS, POOL_GC, POOL_GC), POOL_GC ** -0.5),
        "pool_scale": gain(ks[8], (L, POOL_W)),
        "w_out": nrm(ks[9], (L, D_MIX, D), D_MIX ** -0.5),
        "norm2_g": gain(ks[10], (L, D)),
        "router_group_w": nrm(ks[11], (L, D, MOE_GROUPS), D ** -0.5),
        "router_group_b": nrm(ks[12], (L, MOE_GROUPS), 0.01),
        "router_exp_w": nrm(ks[13], (L, D, N_EXPERTS), D ** -0.5),
        "router_exp_b": nrm(ks[14], (L, N_EXPERTS), 0.01),
        "exp_w_gate": nrm(ks[15], (L, N_EXPERTS, D, D_EXPERT), D ** -0.5),
        "exp_w_up": nrm(ks[16], (L, N_EXPERTS, D, D_EXPERT), D ** -0.5),
        "exp_w_down": nrm(ks[17], (L, N_EXPERTS, D_EXPERT, D), D_EXPERT ** -0.5),
        "ple_norm_g": gain(ks[18], (L, D)),
        "ple_gate_w": nrm(ks[19], (L, D, D), D ** -0.5),
        "ple_gate_b": nrm(ks[20], (L, D), 0.01),
        "ple_proj_w": nrm(ks[21], (L, PLE_DIM, D), PLE_DIM ** -0.5),
        "final_norm_g": gain(ks[22], (D,)),
    }


def reference(x, p, norm1_g, w_in, gla_w_lr2, gla_b_lr, gla_norm_g, pool_w, pool_scale, w_out,
              norm2_g, router_group_w, router_group_b, router_exp_w, router_exp_b,
              exp_w_gate, exp_w_up, exp_w_down, ple_norm_g, ple_gate_w, ple_gate_b, ple_proj_w,
              final_norm_g):
    h = x
    for i in range(DEPTH):
        hn = rmsnorm(h, norm1_g[i])
        proj = hn @ w_in[i]
        parts = []
        off = 0
        for sz in PROJ_SIZES:
            parts.append(proj[..., off:off + sz])
            off += sz
        g_q, g_k, g_v, g_o, g_lr, s_q, s_k, s_v, pool_u = parts
        o_gla = gla_mixer(g_q, g_k, g_v, g_o, g_lr, gla_w_lr2[i], gla_b_lr[i], gla_norm_g[i])
        o_sb = stick_breaking(s_q, s_k, s_v)
        o_pool = pool_mixer(pool_u, pool_w[i], pool_scale[i])
        mix = jnp.concatenate([o_gla, o_sb, o_pool], axis=-1)
        h = h + mix @ w_out[i]
        h = h + hier_moe(rmsnorm(h, norm2_g[i]), router_group_w[i], router_group_b[i],
                         router_exp_w[i], router_exp_b[i], exp_w_gate[i], exp_w_up[i], exp_w_down[i])
        gate = jax.nn.sigmoid((rmsnorm(h, ple_norm_g[i]) @ ple_gate_w[i] + ple_gate_b[i]).astype(jnp.float32))
        e = (p[i] @ ple_proj_w[i]).astype(jnp.float32)
        h = h + (gate * e).astype(h.dtype)
    return rmsnorm(h, final_norm_g)
```

```python
import functools

import jax
import jax.numpy as jnp
from jax import lax
from jax.experimental import pallas as pl
from jax.experimental.pallas import tpu as pltpu

F32 = jnp.float32
BF16 = jnp.bfloat16
I32 = jnp.int32

D_MODEL = 1024
BATCH = 4
SEQ = 4096
DEPTH = 2
TOKENS = BATCH * SEQ
EPS = 1e-6

GLA_HEADS = 4
GLA_HDK = 48
GLA_HDV = 96
GLA_DK = GLA_HEADS * GLA_HDK
GLA_DV = GLA_HEADS * GLA_HDV
GLA_LR = 16
GLA_TAU = 16.0
GLA_CHUNK = 64
SB_HEADS = 6
SB_HD = 64
SB_W = SB_HEADS * SB_HD
POOL_WINDOWS = (2, 4, 8, 16)
POOL_GC = 64
POOL_W = 256
POOL_MAX = 16
MOE_GROUPS = 4
MOE_EPG = 8
N_EXPERTS = MOE_GROUPS * MOE_EPG
D_EXPERT = 256
PLE_DIM = 256

LANES = 128
GLA_KPAD = 256
LR_PAD = LANES
COL_GQ = 0
COL_GK = COL_GQ + GLA_KPAD
COL_GV = COL_GK + GLA_KPAD
COL_GO = COL_GV + GLA_DV
COL_LR = COL_GO + GLA_DV
GLA_COLS = COL_LR + LR_PAD
COL_SQ = GLA_COLS
COL_SK = COL_SQ + SB_W
COL_SV = COL_SK + SB_W
COL_PU = COL_SV + SB_W
PROJ_COLS = COL_PU + POOL_W

ROW_TILE = 256
GLA_TILE = 256
SB_TILE = 256
MOE_TILE = 256
MOE_ROWS = 2 * TOKENS + N_EXPERTS * MOE_TILE
MOE_NT = MOE_ROWS // MOE_TILE
VMEM_LIMIT = 48 * 1024 * 1024


def _split(x):
    hi = x.astype(BF16)
    lo = (x - hi.astype(F32)).astype(BF16)
    return hi, lo


def _dot(a, b):
    return jnp.dot(a, b, preferred_element_type=F32)


def _dot_nt(a, b):
    return lax.dot_general(a, b, (((1,), (1,)), ((), ())), preferred_element_type=F32)


def _dot_tn(a, b):
    return lax.dot_general(a, b, (((0,), (0,)), ((), ())), preferred_element_type=F32)


def _softplus(z):
    return jnp.maximum(z, 0.0) + jnp.log1p(jnp.exp(-jnp.abs(z)))


def _rms(x, g):
    ms = jnp.mean(x * x, axis=-1, keepdims=True)
    return x * lax.rsqrt(ms + EPS) * g


def _inproj_kernel(h_ref, g_ref, w_ref, o_ref):
    hn = _rms(h_ref[...], g_ref[...])
    o_ref[...] = _dot(hn.astype(BF16), w_ref[...])


def _inproj(h, g, w):
    return pl.pallas_call(
        _inproj_kernel,
        out_shape=jax.ShapeDtypeStruct((TOKENS, PROJ_COLS), F32),
        grid=(TOKENS // ROW_TILE,),
        in_specs=[
            pl.BlockSpec((ROW_TILE, D_MODEL), lambda i: (i, 0)),
            pl.BlockSpec((1, D_MODEL), lambda i: (0, 0)),
            pl.BlockSpec((D_MODEL, PROJ_COLS), lambda i: (0, 0)),
        ],
        out_specs=pl.BlockSpec((ROW_TILE, PROJ_COLS), lambda i: (i, 0)),
        compiler_params=pltpu.CompilerParams(
            dimension_semantics=("arbitrary",), vmem_limit_bytes=VMEM_LIMIT),
        name="inproj",
    )(h, g, w)


def _gla_kernel(p_ref, wlr_hi_ref, wlr_lo_ref, blr_ref, ng_ref, o_ref, s_ref):
    @pl.when(pl.program_id(1) == 0)
    def _():
        s_ref[...] = jnp.zeros_like(s_ref)

    n = GLA_TILE
    q = p_ref[:, COL_GQ:COL_GQ + GLA_KPAD]
    k = p_ref[:, COL_GK:COL_GK + GLA_KPAD]
    v = p_ref[:, COL_GV:COL_GV + GLA_DV]
    go = p_ref[:, COL_GO:COL_GO + GLA_DV]
    lr = p_ref[:, COL_LR:COL_LR + LR_PAD]

    lr_hi, lr_lo = _split(lr)
    pre = (_dot(lr_hi, wlr_hi_ref[...]) + _dot(lr_lo, wlr_hi_ref[...])
           + _dot(lr_hi, wlr_lo_ref[...]) + blr_ref[...])
    la = (jnp.minimum(pre, 0.0) - jnp.log1p(jnp.exp(-jnp.abs(pre)))) * (1.0 / GLA_TAU)
    la_hi, la_lo = _split(la)

    r = lax.broadcasted_iota(I32, (n, n), 0)
    c = lax.broadcasted_iota(I32, (n, n), 1)
    tri = ((r >> 6) == (c >> 6)) & (c <= r)
    tri_bf = jnp.where(tri, 1.0, 0.0).astype(BF16)
    b = _dot(tri_bf, la_hi) + _dot(tri_bf, la_lo)
    n_chunks = n // GLA_CHUNK
    b_last = jnp.concatenate(
        [jnp.broadcast_to(b[GLA_CHUNK * (ci + 1) - 1:GLA_CHUNK * (ci + 1), :], (GLA_CHUNK, GLA_KPAD))
         for ci in range(n_chunks)], axis=0)
    q_dec = q * (GLA_HDK ** -0.5) * jnp.exp(b)
    k_in = (k * jnp.exp(-b)).astype(BF16)
    k_out = (k * jnp.exp(b_last - b)).astype(BF16)
    v_bf = v.astype(BF16)

    lane_k = lax.broadcasted_iota(I32, (1, GLA_KPAD), 1)
    lane_v = lax.broadcasted_iota(I32, (1, GLA_DV), 1)
    q_stack = jnp.concatenate(
        [jnp.where((lane_k >= GLA_HDK * h) & (lane_k < GLA_HDK * (h + 1)), q_dec, 0.0)
         for h in range(GLA_HEADS)], axis=0).astype(BF16)
    att = _dot_nt(q_stack, k_in)
    tri4 = jnp.concatenate([tri] * GLA_HEADS, axis=0)
    att = jnp.where(tri4, att, 0.0).astype(BF16)
    oi = _dot(att, v_bf)
    o_intra = jnp.zeros((n, GLA_DV), F32)
    for h in range(GLA_HEADS):
        vm = (lane_v >= GLA_HDV * h) & (lane_v < GLA_HDV * (h + 1))
        o_intra = o_intra + jnp.where(vm, oi[n * h:n * (h + 1)], 0.0)

    row_s = lax.broadcasted_iota(I32, (GLA_KPAD, GLA_DV), 0)
    col_s = lax.broadcasted_iota(I32, (GLA_KPAD, GLA_DV), 1)
    bd = jnp.zeros((GLA_KPAD, GLA_DV), jnp.bool_)
    for h in range(GLA_HEADS):
        bd = bd | ((row_s >= GLA_HDK * h) & (row_s < GLA_HDK * (h + 1))
                   & (col_s >= GLA_HDV * h) & (col_s < GLA_HDV * (h + 1)))
    ones_bf = jnp.ones((GLA_CHUNK, GLA_DV), BF16)
    q_dec_bf = q_dec.astype(BF16)
    state = s_ref[...]
    o_inter = []
    for ci in range(n_chunks):
        sl = slice(GLA_CHUNK * ci, GLA_CHUNK * (ci + 1))
        o_inter.append(_dot(q_dec_bf[sl], state.astype(BF16)))
        kv = _dot_tn(k_out[sl], v_bf[sl])
        tot = _dot_tn(la_hi[sl], ones_bf) + _dot_tn(la_lo[sl], ones_bf)
        state = jnp.exp(tot) * state + jnp.where(bd, kv, 0.0)
    s_ref[...] = state
    o = o_intra + jnp.concatenate(o_inter, axis=0)

    rv = lax.broadcasted_iota(I32, (GLA_DV, GLA_DV), 0)
    cv = lax.broadcasted_iota(I32, (GLA_DV, GLA_DV), 1)
    same_head = jnp.zeros((GLA_DV, GLA_DV), jnp.bool_)
    for h in range(GLA_HEADS):
        same_head = same_head | ((rv >= GLA_HDV * h) & (rv < GLA_HDV * (h + 1))
                                 & (cv >= GLA_HDV * h) & (cv < GLA_HDV * (h + 1)))
    ones_head = jnp.where(same_head, 1.0, 0.0).astype(BF16)
    sq_hi, sq_lo = _split(o * o)
    ms = (_dot(sq_hi, ones_head) + _dot(sq_lo, ones_head)) * (1.0 / GLA_HDV)
    y = o * lax.rsqrt(ms + EPS) * ng_ref[...]
    o_ref[...] = (y * (go / (1.0 + jnp.exp(-go)))).astype(o_ref.dtype)


def _gla(proj, wlr_hi, wlr_lo, blr, ng):
    tiles = SEQ // GLA_TILE
    return pl.pallas_call(
        _gla_kernel,
        out_shape=jax.ShapeDtypeStruct((TOKENS, GLA_DV), BF16),
        grid=(BATCH, tiles),
        in_specs=[
            pl.BlockSpec((GLA_TILE, GLA_COLS), lambda b, i: (b * tiles + i, 0)),
            pl.BlockSpec((LR_PAD, GLA_KPAD), lambda b, i: (0, 0)),
            pl.BlockSpec((LR_PAD, GLA_KPAD), lambda b, i: (0, 0)),
            pl.BlockSpec((1, GLA_KPAD), lambda b, i: (0, 0)),
            pl.BlockSpec((1, GLA_DV), lambda b, i: (0, 0)),
        ],
        out_specs=pl.BlockSpec((GLA_TILE, GLA_DV), lambda b, i: (b * tiles + i, 0)),
        scratch_shapes=[pltpu.VMEM((GLA_KPAD, GLA_DV), F32)],
        compiler_params=pltpu.CompilerParams(
            dimension_semantics=("arbitrary", "arbitrary"), vmem_limit_bytes=VMEM_LIMIT),
        name="gla",
    )(proj, wlr_hi, wlr_lo, blr, ng)


def _sb_kernel(q_ref, k_ref, v_ref, o_ref):
    qi = pl.program_id(2)
    n = SB_TILE
    lane = lax.broadcasted_iota(I32, (1, LANES), 1)
    r = lax.broadcasted_iota(I32, (n, n), 0)
    c = lax.broadcasted_iota(I32, (n, n), 1)
    tri_incl = jnp.where(r >= c, 1.0, 0.0).astype(BF16)
    below = c < r
    q = q_ref[...] * (SB_HD ** -0.5)

    def tile(qh, j, carry, acc, diag):
        start = pl.multiple_of(j * n, n)
        kj = k_ref[pl.ds(start, n), :].astype(BF16)
        vj = v_ref[pl.ds(start, n), :].astype(BF16)
        z = _dot_nt(qh, kj)
        sp = _softplus(z)
        if diag:
            sp = jnp.where(below, sp, 0.0)
        sp_hi, sp_lo = _split(sp)
        cs = _dot(sp_hi, tri_incl) + _dot(sp_lo, tri_incl)
        a = jnp.exp(z - cs - carry)
        if diag:
            a = jnp.where(below, a, 0.0)
        acc = acc + _dot(a.astype(BF16), vj)
        carry = carry + cs[:, 0:1]
        return carry, acc

    outs = []
    for hd in range(2):
        hm = (lane >= SB_HD * hd) & (lane < SB_HD * (hd + 1))
        qh = jnp.where(hm, q, 0.0).astype(BF16)
        carry, acc = tile(qh, qi, jnp.zeros((n, 1), F32), jnp.zeros((n, LANES), F32), True)

        def body(t, ca, qh=qh):
            return tile(qh, qi - 1 - t, ca[0], ca[1], False)

        carry, acc = lax.fori_loop(0, qi, body, (carry, acc))
        outs.append(acc)
    o_ref[...] = jnp.where(lane < SB_HD, outs[0], outs[1]).astype(o_ref.dtype)


def _stick_breaking(proj):
    tiles = SEQ // SB_TILE
    cq, ck, cv = COL_SQ // LANES, COL_SK // LANES, COL_SV // LANES
    return pl.pallas_call(
        _sb_kernel,
        out_shape=jax.ShapeDtypeStruct((TOKENS, SB_W), BF16),
        grid=(BATCH, SB_HEADS // 2, tiles),
        in_specs=[
            pl.BlockSpec((SB_TILE, LANES), lambda b, p, i: (b * tiles + i, cq + p)),
            pl.BlockSpec((SEQ, LANES), lambda b, p, i: (b, ck + p)),
            pl.BlockSpec((SEQ, LANES), lambda b, p, i: (b, cv + p)),
        ],
        out_specs=pl.BlockSpec((SB_TILE, LANES), lambda b, p, i: (b * tiles + i, p)),
        compiler_params=pltpu.CompilerParams(
            dimension_semantics=("arbitrary", "arbitrary", "arbitrary"),
            vmem_limit_bytes=VMEM_LIMIT),
        name="stick_breaking",
    )(proj, proj, proj)


def _mix_router_kernel(h_ref, og_ref, os_ref, pu_ref, pp_ref, pw_ref, ps_ref, wo_ref, g2_ref,
                       wr_hi_ref, wr_lo_ref, br_ref,
                       h2_ref, hn_ref, route_ref, cnt_ref):
    i = pl.program_id(0)
    n = ROW_TILE
    tiles_per_seq = SEQ // n

    @pl.when(i == 0)
    def _():
        cnt_ref[...] = jnp.zeros_like(cnt_ref)

    seq_start = (i % tiles_per_seq) == 0
    u = pu_ref[...]
    prev = jnp.where(seq_start, 0.0, pp_ref[...])
    ext = jnp.concatenate([prev, u], axis=0)
    sums = []
    acc = ext
    for w in POOL_WINDOWS:
        acc = acc + pltpu.roll(acc, w // 2, axis=0)
        sums.append(acc[POOL_MAX:, :])
    lane_p = lax.broadcasted_iota(I32, (1, POOL_W), 1)
    pos = ((i % tiles_per_seq) * n + lax.broadcasted_iota(I32, (n, 1), 0)).astype(F32)
    pooled = jnp.zeros((n, POOL_W), F32)
    for gi, w in enumerate(POOL_WINDOWS):
        cnt = jnp.minimum(pos + 1.0, float(w))
        val = sums[gi] / cnt - u
        pooled = jnp.where((lane_p >= POOL_GC * gi) & (lane_p < POOL_GC * (gi + 1)), val, pooled)
    o_pool = _dot(pooled.astype(BF16), pw_ref[...]) * ps_ref[...]

    mix = (_dot(og_ref[...], wo_ref[0:GLA_DV, :])
           + _dot(os_ref[...], wo_ref[GLA_DV:GLA_DV + SB_W, :])
           + _dot(o_pool.astype(BF16), wo_ref[GLA_DV + SB_W:, :]))
    h2 = h_ref[...] + mix
    h2_ref[...] = h2
    hn = _rms(h2, g2_ref[...])
    hn_ref[...] = hn

    hn_hi, hn_lo = _split(hn)
    logits = (_dot(hn_hi, wr_hi_ref[...]) + _dot(hn_lo, wr_hi_ref[...])
              + _dot(hn_hi, wr_lo_ref[...]) + br_ref[...])
    lane = lax.broadcasted_iota(I32, (n, LANES), 1)
    neg = -jnp.inf
    big = LANES
    is_g = lane < MOE_GROUPS
    gl = jnp.where(is_g, logits, neg)
    gmax = jnp.max(gl, axis=-1, keepdims=True)
    gidx = jnp.min(jnp.where(gl == gmax, lane, big), axis=-1, keepdims=True)
    gz = jnp.sum(jnp.where(is_g, jnp.exp(gl - gmax), 0.0), axis=-1, keepdims=True)
    g_w = 1.0 / gz
    lo_lane = MOE_GROUPS + gidx * MOE_EPG
    in_grp = (lane >= lo_lane) & (lane < lo_lane + MOE_EPG)
    el = jnp.where(in_grp, logits, neg)
    m1 = jnp.max(el, axis=-1, keepdims=True)
    i1 = jnp.min(jnp.where(el == m1, lane, big), axis=-1, keepdims=True)
    el2 = jnp.where(lane == i1, neg, el)
    m2 = jnp.max(el2, axis=-1, keepdims=True)
    i2 = jnp.min(jnp.where(el2 == m2, lane, big), axis=-1, keepdims=True)
    u2 = jnp.exp(m2 - m1)
    w1 = g_w / (1.0 + u2)
    w2 = g_w * u2 / (1.0 + u2)
    e1 = i1 - MOE_GROUPS
    e2 = i2 - MOE_GROUPS

    oh = jnp.concatenate([jnp.where(lane == e1, 1.0, 0.0), jnp.where(lane == e2, 1.0, 0.0)], axis=0)
    rr = lax.broadcasted_iota(I32, (2 * n, 2 * n), 0)
    cc = lax.broadcasted_iota(I32, (2 * n, 2 * n), 1)
    strict = jnp.where(cc < rr, 1.0, 0.0).astype(BF16)
    before = _dot(strict, oh.astype(BF16)) + cnt_ref[...]
    rank = jnp.sum(jnp.where(oh > 0.0, before, 0.0), axis=-1, keepdims=True)
    cnt_ref[...] = cnt_ref[...] + jnp.sum(oh, axis=0, keepdims=True)

    route = jnp.zeros((n, LANES), F32)
    for li, val in enumerate((e1.astype(F32), e2.astype(F32), w1, w2, rank[:n], rank[n:])):
        route = jnp.where(lane == li, val, route)
    route_ref[...] = route


def _mix_router(h, og, osb, proj, pw_bd, ps, wo, g2, wr_hi, wr_lo, br):
    n = ROW_TILE
    halo_blocks = n // POOL_MAX
    cpu = COL_PU // POOL_W
    const = lambda i: (0, 0)
    return pl.pallas_call(
        _mix_router_kernel,
        out_shape=(
            jax.ShapeDtypeStruct((TOKENS, D_MODEL), F32),
            jax.ShapeDtypeStruct((TOKENS, D_MODEL), F32),
            jax.ShapeDtypeStruct((TOKENS, LANES), F32),
            jax.ShapeDtypeStruct((1, LANES), F32),
        ),
        grid=(TOKENS // n,),
        in_specs=[
            pl.BlockSpec((n, D_MODEL), lambda i: (i, 0)),
            pl.BlockSpec((n, GLA_DV), lambda i: (i, 0)),
            pl.BlockSpec((n, SB_W), lambda i: (i, 0)),
            pl.BlockSpec((n, POOL_W), lambda i: (i, cpu)),
            pl.BlockSpec((POOL_MAX, POOL_W), lambda i: (jnp.maximum(i * halo_blocks - 1, 0), cpu)),
            pl.BlockSpec((POOL_W, POOL_W), const),
            pl.BlockSpec((1, POOL_W), const),
            pl.BlockSpec((D_MODEL, D_MODEL), const),
            pl.BlockSpec((1, D_MODEL), const),
            pl.BlockSpec((D_MODEL, LANES), const),
            pl.BlockSpec((D_MODEL, LANES), const),
            pl.BlockSpec((1, LANES), const),
        ],
        out_specs=(
            pl.BlockSpec((n, D_MODEL), lambda i: (i, 0)),
            pl.BlockSpec((n, D_MODEL), lambda i: (i, 0)),
            pl.BlockSpec((n, LANES), lambda i: (i, 0)),
            pl.BlockSpec((1, LANES), const),
        ),
        compiler_params=pltpu.CompilerParams(
            dimension_semantics=("arbitrary",), vmem_limit_bytes=VMEM_LIMIT),
        name="mix_router",
    )(h, og, osb, proj, proj, pw_bd, ps, wo, g2, wr_hi, wr_lo, br)


def _row_copy(src_ref, src_row, dst_ref, dst_row, sem):
    return pltpu.make_async_copy(src_ref.at[pl.ds(src_row, 1)], dst_ref.at[pl.ds(dst_row, 1)], sem)


def _scatter_kernel(pos_ref, x_ref, init_ref, xs_ref, sem):
    del init_ref
    base = pl.program_id(0) * ROW_TILE

    def issue(r, carry):
        for kk in range(2):
            _row_copy(x_ref, r, xs_ref, pos_ref[kk * TOKENS + base + r], sem).start()
        return carry

    lax.fori_loop(0, ROW_TILE, issue, 0)

    def drain(r, carry):
        for kk in range(2):
            _row_copy(x_ref, r, xs_ref, pos_ref[kk * TOKENS + base + r], sem).wait()
        return carry

    lax.fori_loop(0, ROW_TILE, drain, 0)


def _scatter_rows(pos, x, init):
    return pl.pallas_call(
        _scatter_kernel,
        out_shape=jax.ShapeDtypeStruct((MOE_ROWS, D_MODEL), F32),
        grid_spec=pltpu.PrefetchScalarGridSpec(
            num_scalar_prefetch=1,
            grid=(TOKENS // ROW_TILE,),
            in_specs=[
                pl.BlockSpec((ROW_TILE, D_MODEL), lambda i, pos: (i, 0)),
                pl.BlockSpec(memory_space=pl.ANY),
            ],
            out_specs=pl.BlockSpec(memory_space=pl.ANY),
            scratch_shapes=[pltpu.SemaphoreType.DMA],
        ),
        input_output_aliases={2: 0},
        compiler_params=pltpu.CompilerParams(
            dimension_semantics=("arbitrary",), vmem_limit_bytes=VMEM_LIMIT),
        name="moe_scatter",
    )(pos, x, init)


def _ffn_kernel(te_ref, nu_ref, xs_ref, wg_ref, wu_ref, wd_ref, ys_ref):
    i = pl.program_id(0)

    @pl.when(i < nu_ref[0])
    def _():
        x = xs_ref[...].astype(BF16)
        a = _dot(x, wg_ref[...])
        b = _dot(x, wu_ref[...])
        hid = a / (1.0 + jnp.exp(-a)) * b
        ys_ref[...] = _dot(hid.astype(BF16), wd_ref[...])

    @pl.when(i >= nu_ref[0])
    def _():
        ys_ref[...] = jnp.zeros_like(ys_ref)


def _expert_ffn(tile_expert, n_used, xs, wg, wu, wd):
    def x_map(i, te, nu):
        return (jnp.minimum(i, nu[0] - 1), 0)

    def w_map(i, te, nu):
        return (te[i], 0, 0)

    return pl.pallas_call(
        _ffn_kernel,
        out_shape=jax.ShapeDtypeStruct((MOE_ROWS, D_MODEL), F32),
        grid_spec=pltpu.PrefetchScalarGridSpec(
            num_scalar_prefetch=2,
            grid=(MOE_NT,),
            in_specs=[
                pl.BlockSpec((MOE_TILE, D_MODEL), x_map),
                pl.BlockSpec((None, D_MODEL, D_EXPERT), w_map),
                pl.BlockSpec((None, D_MODEL, D_EXPERT), w_map),
                pl.BlockSpec((None, D_EXPERT, D_MODEL), w_map),
            ],
            out_specs=pl.BlockSpec((MOE_TILE, D_MODEL), lambda i, te, nu: (i, 0)),
        ),
        compiler_params=pltpu.CompilerParams(
            dimension_semantics=("arbitrary",), vmem_limit_bytes=VMEM_LIMIT),
        name="moe_ffn",
    )(tile_expert, n_used, xs, wg, wu, wd)


def _combine_kernel(pos_ref, h2_ref, route_ref, ys_ref, p_ref, png_ref, wgate_ref, bgate_ref,
                    wproj_ref, fg_ref, o_ref, ybuf, sem, *, final):
    base = pl.program_id(0) * ROW_TILE

    def issue(r, carry):
        for kk in range(2):
            _row_copy(ys_ref, pos_ref[kk * TOKENS + base + r], ybuf.at[kk], r, sem).start()
        return carry

    lax.fori_loop(0, ROW_TILE, issue, 0)

    def drain(r, carry):
        for kk in range(2):
            _row_copy(ys_ref, pos_ref[kk * TOKENS + base + r], ybuf.at[kk], r, sem).wait()
        return carry

    lax.fori_loop(0, ROW_TILE, drain, 0)

    route = route_ref[...]
    w1 = route[:, 2:3]
    w2 = route[:, 3:4]
    h3 = h2_ref[...] + w1 * ybuf[0] + w2 * ybuf[1]
    hn = _rms(h3, png_ref[...])
    gate_pre = _dot(hn.astype(BF16), wgate_ref[...]) + bgate_ref[...]
    gate = 1.0 / (1.0 + jnp.exp(-gate_pre))
    e = _dot(p_ref[...].astype(BF16), wproj_ref[...])
    h4 = h3 + gate * e
    if final:
        h4 = _rms(h4, fg_ref[...])
    o_ref[...] = h4


def _combine(pos, h2, route, ys, p, png, wgate, bgate, wproj, fg, final):
    n = ROW_TILE
    const = lambda i, pos: (0, 0)
    return pl.pallas_call(
        functools.partial(_combine_kernel, final=final),
        out_shape=jax.ShapeDtypeStruct((TOKENS, D_MODEL), F32),
        grid_spec=pltpu.PrefetchScalarGridSpec(
            num_scalar_prefetch=1,
            grid=(TOKENS // n,),
            in_specs=[
                pl.BlockSpec((n, D_MODEL), lambda i, pos: (i, 0)),
                pl.BlockSpec((n, LANES), lambda i, pos: (i, 0)),
                pl.BlockSpec(memory_space=pl.ANY),
                pl.BlockSpec((n, PLE_DIM), lambda i, pos: (i, 0)),
                pl.BlockSpec((1, D_MODEL), const),
                pl.BlockSpec((D_MODEL, D_MODEL), const),
                pl.BlockSpec((1, D_MODEL), const),
                pl.BlockSpec((PLE_DIM, D_MODEL), const),
                pl.BlockSpec((1, D_MODEL), const),
            ],
            out_specs=pl.BlockSpec((n, D_MODEL), lambda i, pos: (i, 0)),
            scratch_shapes=[pltpu.VMEM((2, n, D_MODEL), F32), pltpu.SemaphoreType.DMA],
        ),
        compiler_params=pltpu.CompilerParams(
            dimension_semantics=("arbitrary",), vmem_limit_bytes=VMEM_LIMIT),
        name="moe_combine_ple",
    )(pos, h2, route, ys, p, png, wgate, bgate, wproj, fg)


def _pad_cols(w, width):
    return jnp.pad(w, ((0, 0), (0, width - w.shape[1])))


def _pack_w_in(w):
    sizes = (GLA_DK, GLA_DK, GLA_DV, GLA_DV, GLA_LR, SB_W, SB_W, SB_W, POOL_W)
    widths = (GLA_KPAD, GLA_KPAD, GLA_DV, GLA_DV, LR_PAD, SB_W, SB_W, SB_W, POOL_W)
    parts, off = [], 0
    for sz, wd in zip(sizes, widths):
        parts.append(_pad_cols(w[:, off:off + sz], wd))
        off += sz
    return jnp.concatenate(parts, axis=1).astype(BF16)


def _split_host(w):
    hi = w.astype(BF16)
    return hi, (w - hi.astype(F32)).astype(BF16)


def _routing_tables(route, counts):
    e = route[:, 0:2].astype(I32)
    rank = route[:, 4:6].astype(I32)
    cnt = counts[0, :N_EXPERTS].astype(I32)
    padded = ((cnt + MOE_TILE - 1) // MOE_TILE) * MOE_TILE
    ends = jnp.cumsum(padded)
    offs = ends - padded
    pos = (offs[e] + rank).T.reshape(-1)
    tile_start = jnp.arange(MOE_NT, dtype=I32) * MOE_TILE
    tile_expert = jnp.minimum(jnp.searchsorted(ends, tile_start, side="right"), N_EXPERTS - 1).astype(I32)
    n_used = (ends[-1:] // MOE_TILE).astype(I32)
    return pos, tile_expert, n_used


def kernel(x, p, norm1_g, w_in, gla_w_lr2, gla_b_lr, gla_norm_g, pool_w, pool_scale, w_out, norm2_g, router_group_w, router_group_b, router_exp_w, router_exp_b, exp_w_gate, exp_w_up, exp_w_down, ple_norm_g, ple_gate_w, ple_gate_b, ple_proj_w, final_norm_g):
    h = x.reshape(TOKENS, D_MODEL)
    row = lambda v: v.reshape(1, -1)
    xs_init = jnp.zeros((MOE_ROWS, D_MODEL), F32)
    for li in range(DEPTH):
        proj = _inproj(h, row(norm1_g[li]), _pack_w_in(w_in[li]))

        wlr = jnp.pad(gla_w_lr2[li], ((0, LR_PAD - GLA_LR), (0, GLA_KPAD - GLA_DK)))
        wlr_hi, wlr_lo = _split_host(wlr)
        blr = _pad_cols(row(gla_b_lr[li]), GLA_KPAD)
        ng = row(jnp.tile(gla_norm_g[li], GLA_HEADS))
        o_gla = _gla(proj, wlr_hi, wlr_lo, blr, ng)
        o_sb = _stick_breaking(proj)

        pw_bd = jax.scipy.linalg.block_diag(*[pool_w[li, g] for g in range(len(POOL_WINDOWS))]).astype(BF16)
        wr = _pad_cols(jnp.concatenate([router_group_w[li], router_exp_w[li]], axis=1), LANES)
        wr_hi, wr_lo = _split_host(wr)
        br = _pad_cols(row(jnp.concatenate([router_group_b[li], router_exp_b[li]])), LANES)
        h2, hn2, route, counts = _mix_router(
            h, o_gla, o_sb, proj, pw_bd, row(pool_scale[li]), w_out[li].astype(BF16),
            row(norm2_g[li]), wr_hi, wr_lo, br)

        pos, tile_expert, n_used = _routing_tables(route, counts)
        xs = _scatter_rows(pos, hn2, xs_init)
        ys = _expert_ffn(tile_expert, n_used, xs, exp_w_gate[li].astype(BF16),
                         exp_w_up[li].astype(BF16), exp_w_down[li].astype(BF16))
        h = _combine(pos, h2, route, ys, p[li].reshape(TOKENS, PLE_DIM), row(ple_norm_g[li]),
                     ple_gate_w[li].astype(BF16), row(ple_gate_b[li]), ple_proj_w[li].astype(BF16),
                     row(final_norm_g), li == DEPTH - 1)
    return h.reshape(BATCH, SEQ, D_MODEL)
```
